```python
import math
import jax
import jax.numpy as jnp
from jax import lax
import numpy as np

D_MODEL = 1024
BATCH = 16
SEQ = 256
DEPTH = 2
DEC_BATCH = 2
DEC_SEQ = 1024
PAST_LEN = 256

GRID_W = 64
N_MIXERS = 2
N_ATTN_LAYERS = (DEPTH + N_MIXERS - 1) // N_MIXERS
N_CONV_LAYERS = DEPTH // N_MIXERS
N_HEADS = 8
HEAD_DIM = 64
V_DIM = 2 * HEAD_DIM
ROPE_THETA = 10000.0
Q_BLOCK = 128
CONV_WIDTH = 31
N_EXPERTS = 16
N_GROUPS = 4
EXPERTS_PER_GROUP = N_EXPERTS // N_GROUPS
TOP_K = 2
D_EXPERT = 512
EPS = 1e-6

kernel_name = 'hybrid_diffattn_conformer_moe_prefix_dit_step'


def rmsnorm(x, g):
    xf = x.astype(jnp.float32)
    y = xf * lax.rsqrt(jnp.mean(xf * xf, axis=-1, keepdims=True) + EPS)
    return (y * g.astype(jnp.float32)).astype(x.dtype)


def layernorm(x, g, b):
    xf = x.astype(jnp.float32)
    mu = jnp.mean(xf, axis=-1, keepdims=True)
    var = jnp.mean(jnp.square(xf - mu), axis=-1, keepdims=True)
    y = (xf - mu) * lax.rsqrt(var + EPS)
    return (y * g.astype(jnp.float32) + b.astype(jnp.float32)).astype(x.dtype)


def modulation(cond, w_ada, b_ada):
    m = jax.nn.silu(cond) @ w_ada + b_ada
    return jnp.split(m, 6, axis=-1)


def modulate(h, shift, scale):
    return h * (1.0 + scale) + shift


def axial_rope_tables(n_tokens):
    n_rows = n_tokens // GRID_W
    row = jnp.repeat(jnp.arange(n_rows, dtype=jnp.float32), GRID_W)
    col = jnp.tile(jnp.arange(GRID_W, dtype=jnp.float32), n_rows)
    n_freq = HEAD_DIM // 4
    inv_freq = ROPE_THETA ** (-jnp.arange(n_freq, dtype=jnp.float32) / n_freq)
    ang = jnp.stack([row[:, None] * inv_freq, col[:, None] * inv_freq], axis=1)
    return jnp.cos(ang), jnp.sin(ang)


def apply_axial_rope(x, cos, sin):
    shp = x.shape
    xs = x.astype(jnp.float32).reshape(shp[:-1] + (2, 2, HEAD_DIM // 4))
    x1, x2 = xs[..., 0, :], xs[..., 1, :]
    c = cos[None, :, None, None]
    s = sin[None, :, None, None]
    out = jnp.stack([x1 * c - x2 * s, x1 * s + x2 * c], axis=-2)
    return out.reshape(shp).astype(x.dtype)


def diff_attn_project(h, w_qkv):
    b, t, _ = h.shape
    q, k, v = jnp.split(h @ w_qkv, 3, axis=-1)
    q = q.reshape(b, t, N_HEADS, 2, HEAD_DIM)
    k = k.reshape(b, t, N_HEADS, 2, HEAD_DIM)
    v = v.reshape(b, t, N_HEADS, V_DIM)
    return q, k, v


def diff_attention_core(q, k, v, lam):
    b, tq = q.shape[:2]
    nb = tq // Q_BLOCK
    qb = q.reshape(b, nb, Q_BLOCK, N_HEADS, 2, HEAD_DIM).swapaxes(0, 1)
    scale = HEAD_DIM ** -0.5

    def one_block(qblk):
        s = jnp.einsum('bqhmd,bkhmd->bhmqk', qblk, k, preferred_element_type=jnp.float32) * scale
        p = jax.nn.softmax(s, axis=-1)
        a = p[:, :, 0] - lam * p[:, :, 1]
        return jnp.einsum('bhqk,bkhe->bqhe', a.astype(v.dtype), v)

    out = lax.map(one_block, qb)
    return out.swapaxes(0, 1).reshape(b, tq, N_HEADS, V_DIM)


def diff_attn_output(o, subln_g, w_o, lambda_init):
    b, t = o.shape[:2]
    o = rmsnorm(o, subln_g) * (1.0 - lambda_init)
    return o.reshape(b, t, D_MODEL) @ w_o


def conformer_conv(h, w_pw1, b_pw1, w_dw, b_dw, ln_g, ln_b, w_pw2, b_pw2):
    a, g = jnp.split(h @ w_pw1 + b_pw1, 2, axis=-1)
    u = a * jax.nn.sigmoid(g)
    u = lax.conv_general_dilated(
        u, w_dw[:, None, :].astype(u.dtype), window_strides=(1,),
        padding=[(CONV_WIDTH // 2, CONV_WIDTH // 2)],
        dimension_numbers=('NWC', 'WIO', 'NWC'),
        feature_group_count=D_MODEL) + b_dw
    u = jax.nn.silu(layernorm(u, ln_g, ln_b))
    return u @ w_pw2 + b_pw2


def grouped_moe(h, router_w, router_bias, w_gate, w_up, w_down):
    b, t, d = h.shape
    hf = h.reshape(b * t, d)
    scores = jax.nn.sigmoid(jnp.dot(hf, router_w, preferred_element_type=jnp.float32))
    sel = scores + router_bias.astype(jnp.float32)
    gscore = lax.top_k(sel.reshape(-1, N_GROUPS, EXPERTS_PER_GROUP), TOP_K)[0].sum(-1)
    best = jnp.argmax(gscore, axis=-1)
    in_group = (jnp.arange(N_EXPERTS) // EXPERTS_PER_GROUP)[None, :] == best[:, None]
    _, idx = lax.top_k(jnp.where(in_group, sel, -jnp.inf), TOP_K)
    wts = jnp.take_along_axis(scores, idx, axis=-1)
    wts = wts / jnp.sum(wts, axis=-1, keepdims=True)
    gates = jnp.einsum('nk,nke->ne', wts, jax.nn.one_hot(idx, N_EXPERTS, dtype=jnp.float32))
    gact = jnp.einsum('nd,edf->nef', hf, w_gate)
    up = jnp.einsum('nd,edf->nef', hf, w_up)
    act = jax.nn.silu(gact) * up * gates.astype(h.dtype)[:, :, None]
    y = jnp.einsum('nef,efd->nd', act, w_down)
    return y.reshape(b, t, d)


def setup_inputs(seed: int = 0) -> dict:
    key = jax.random.key(seed)
    ks = list(jax.random.split(key, 40))

    def nrm(i, shape, scale):
        return jax.random.normal(ks[i], shape, jnp.float32) * scale

    D = D_MODEL
    return {
        'x_prompt': nrm(0, (BATCH, SEQ, D), 1.0),
        'x_sample': nrm(1, (DEC_BATCH, DEC_SEQ, D), 1.0),
        'cache_k': nrm(2, (DEC_BATCH, N_ATTN_LAYERS, PAST_LEN, N_HEADS, 2 * HEAD_DIM), 1.0),
        'cache_v': nrm(3, (DEC_BATCH, N_ATTN_LAYERS, PAST_LEN, N_HEADS, V_DIM), 1.0),
        'c': nrm(4, (DEC_BATCH, D), 1.0),
        'c_ctx': nrm(5, (D,), 1.0),
        'norm1_g': 1.0 + nrm(6, (DEPTH, D), 0.02),
        'norm2_g': 1.0 + nrm(7, (DEPTH, D), 0.02),
        'ada_w': nrm(8, (DEPTH, D, 6 * D), 0.5 * D ** -0.5),
        'ada_b': nrm(9, (DEPTH, 6 * D), 0.01),
        'attn_w_qkv': nrm(10, (N_ATTN_LAYERS, D, 3 * D), D ** -0.5),
        'attn_w_o': nrm(11, (N_ATTN_LAYERS, D, D), D ** -0.5),
        'attn_lambda_q1': nrm(12, (N_ATTN_LAYERS, HEAD_DIM), 0.1),
        'attn_lambda_k1': nrm(13, (N_ATTN_LAYERS, HEAD_DIM), 0.1),
        'attn_lambda_q2': nrm(14, (N_ATTN_LAYERS, HEAD_DIM), 0.1),
        'attn_lambda_k2': nrm(15, (N_ATTN_LAYERS, HEAD_DIM), 0.1),
        'attn_subln_g': 1.0 + nrm(16, (N_ATTN_LAYERS, V_DIM), 0.02),
        'conv_w_pw1': nrm(17, (N_CONV_LAYERS, D, 2 * D), D ** -0.5),
        'conv_b_pw1': nrm(18, (N_CONV_LAYERS, 2 * D), 0.01),
        'conv_w_dw': nrm(19, (N_CONV_LAYERS, CONV_WIDTH, D), CONV_WIDTH ** -0.5),
        'conv_b_dw': nrm(20, (N_CONV_LAYERS, D), 0.01),
        'conv_ln_g': 1.0 + nrm(21, (N_CONV_LAYERS, D), 0.02),
        'conv_ln_b': nrm(22, (N_CONV_LAYERS, D), 0.01),
        'conv_w_pw2': nrm(23, (N_CONV_LAYERS, D, D), D ** -0.5),
        'conv_b_pw2': nrm(24, (N_CONV_LAYERS, D), 0.01),
        'router_w': nrm(25, (D, N_EXPERTS), D ** -0.5),
        'router_bias': nrm(26, (N_EXPERTS,), 0.01),
        'moe_w_gate': nrm(27, (DEPTH, N_EXPERTS, D, D_EXPERT), D ** -0.5),
        'moe_w_up': nrm(28, (DEPTH, N_EXPERTS, D, D_EXPERT), D ** -0.5),
        'moe_w_down': nrm(29, (DEPTH, N_EXPERTS, D_EXPERT, D), D_EXPERT ** -0.5),
        'final_norm_g': 1.0 + nrm(30, (D,), 0.02),
    }


def reference(x_prompt, x_sample, cache_k, cache_v, c, c_ctx, norm1_g, norm2_g, ada_w, ada_b,
              attn_w_qkv, attn_w_o, attn_lambda_q1, attn_lambda_k1, attn_lambda_q2, attn_lambda_k2,
              attn_subln_g, conv_w_pw1, conv_b_pw1, conv_w_dw, conv_b_dw, conv_ln_g, conv_ln_b,
              conv_w_pw2, conv_b_pw2, router_w, router_bias, moe_w_gate, moe_w_up, moe_w_down,
              final_norm_g):
    xc = x_prompt
    xl = x_sample
    b_ctx, t_ctx = xc.shape[:2]
    b_lat, t_lat = xl.shape[:2]
    cond_ctx = c_ctx[None, None, :]
    cond_lat = c[:, None, :]
    cos, sin = axial_rope_tables(t_lat)
    new_k, new_v = [], []

    for layer in range(DEPTH):
        mc = modulation(cond_ctx, ada_w[layer], ada_b[layer])
        ml = modulation(cond_lat, ada_w[layer], ada_b[layer])
        hc = modulate(rmsnorm(xc, norm1_g[layer]), mc[0], mc[1])
        hl = modulate(rmsnorm(xl, norm1_g[layer]), ml[0], ml[1])
        j = layer // N_MIXERS
        if layer % N_MIXERS == 0:
            lambda_init = 0.8 - 0.6 * math.exp(-0.3 * layer)
            lam = (jnp.exp(jnp.sum(attn_lambda_q1[j].astype(jnp.float32) * attn_lambda_k1[j].astype(jnp.float32)))
                   - jnp.exp(jnp.sum(attn_lambda_q2[j].astype(jnp.float32) * attn_lambda_k2[j].astype(jnp.float32)))
                   + lambda_init)
            qc, kc, vc = diff_attn_project(hc, attn_w_qkv[j])
            oc = diff_attention_core(qc, kc, vc, lam)
            new_k.append(kc.reshape(b_ctx, t_ctx, N_HEADS, 2 * HEAD_DIM))
            new_v.append(vc)
            ql, kl, vl = diff_attn_project(hl, attn_w_qkv[j])
            ql = apply_axial_rope(ql, cos, sin)
            kl = apply_axial_rope(kl, cos, sin)
            past = cache_k.shape[2]
            k_all = jnp.concatenate(
                [cache_k[:, j].reshape(b_lat, past, N_HEADS, 2, HEAD_DIM).astype(kl.dtype), kl], axis=1)
            v_all = jnp.concatenate([cache_v[:, j].astype(vl.dtype), vl], axis=1)
            ol = diff_attention_core(ql, k_all, v_all, lam)
            mix_c = diff_attn_output(oc, attn_subln_g[j], attn_w_o[j], lambda_init)
            mix_l = diff_attn_output(ol, attn_subln_g[j], attn_w_o[j], lambda_init)
        else:
            conv_args = (conv_w_pw1[j], conv_b_pw1[j], conv_w_dw[j], conv_b_dw[j],
                         conv_ln_g[j], conv_ln_b[j], conv_w_pw2[j], conv_b_pw2[j])
            mix_c = conformer_conv(hc, *conv_args)
            mix_l = conformer_conv(hl, *conv_args)
        xc = xc + mc[2] * mix_c
        xl = xl + ml[2] * mix_l

        hc = modulate(rmsnorm(xc, norm2_g[layer]), mc[3], mc[4])
        hl = modulate(rmsnorm(xl, norm2_g[layer]), ml[3], ml[4])
        xc = xc + mc[5] * grouped_moe(hc, router_w, router_bias, moe_w_gate[layer], moe_w_up[layer], moe_w_down[layer])
        xl = xl + ml[5] * grouped_moe(hl, router_w, router_bias, moe_w_gate[layer], moe_w_up[layer], moe_w_down[layer])

    y_prompt = rmsnorm(xc, final_norm_g)
    y_sample = rmsnorm(xl, final_norm_g)
    new_cache_k = jnp.stack(new_k, axis=1)
    new_cache_v = jnp.stack(new_v, axis=1)
    return (y_prompt, y_sample, new_cache_k, new_cache_v)
```

```python
import functools
import math

import jax
import jax.numpy as jnp
from jax import lax
from jax.experimental import pallas as pl
from jax.experimental.pallas import tpu as pltpu

F32 = jnp.float32
BF16 = jnp.bfloat16
I32 = jnp.int32
HIGHEST = lax.Precision.HIGHEST

D = 1024
DEPTH = 2
N_CTX_SEQ, T_CTX = 16, 256
N_LAT_SEQ, T_LAT = 2, 1024
N_CTX = N_CTX_SEQ * T_CTX
N_LAT = N_LAT_SEQ * T_LAT
N_TOK = N_CTX + N_LAT
PAST = 256
GRID_W = 64
N_HEADS = 8
HEAD_DIM = 64
V_DIM = 128
ROPE_THETA = 10000.0
CONV_W = 31
N_EXPERTS = 16
GROUP = 4
D_EXPERT = 512
EPS = 1e-6

LANES = 128
BF16_SUBLANES = 16

MOD_ROWS = 8
MOD_TN = 1536
TM_QKV = 256
TQ_LAT = 256
BLK = 512
N_BLK = N_TOK // BLK
CHUNK = BF16_SUBLANES
CHUNK_SHIFT = CHUNK.bit_length() - 1
BLK_ROWS = 2 * BLK + N_EXPERTS * CHUNK
BLK_CHUNKS = BLK_ROWS // CHUNK
N_CHUNKS = N_BLK * BLK_CHUNKS
TILE_CHUNKS = 16
MAX_CHUNKS_PER_EXPERT = 400
VMEM_LIMIT = 56 * 1024 * 1024


def _params(sem, vmem=None):
    return pltpu.CompilerParams(dimension_semantics=sem, vmem_limit_bytes=vmem)


def _sigmoid(x):
    return 1.0 / (1.0 + jnp.exp(-x))


def _norm_mod(x, g, shift, scale):
    ms = jnp.mean(x * x, axis=-1, keepdims=True)
    y = (x * lax.rsqrt(ms + EPS)) * g
    return y * (1.0 + scale) + shift


def _mod_row(i, tm):
    nct = N_CTX // tm
    return jnp.where(i < nct, 0, 1 + (i - nct) // (T_LAT // tm))


def _split_specs(tm, width):
    nct = N_CTX // tm
    spec_c = pl.BlockSpec((tm, width), lambda i, *_: (jnp.minimum(i, nct - 1), 0))
    spec_l = pl.BlockSpec((tm, width), lambda i, *_: (jnp.maximum(i - nct, 0), 0))
    return spec_c, spec_l


def _const_spec(shape):
    nd = len(shape)
    return pl.BlockSpec(shape, lambda *_: (0,) * nd)


def _mod_kernel(cond_ref, w_ref, b_ref, o_ref):
    c = cond_ref[...]
    s = c * _sigmoid(c)
    o_ref[...] = jnp.dot(s, w_ref[...], precision=HIGHEST, preferred_element_type=F32) + b_ref[...]


def _modulation(cond, ada_w, ada_b):
    n_out = 6 * D
    out = pl.pallas_call(
        _mod_kernel,
        grid=(DEPTH, n_out // MOD_TN),
        in_specs=[
            _const_spec((MOD_ROWS, D)),
            pl.BlockSpec((None, D, MOD_TN), lambda l, j: (l, 0, j)),
            pl.BlockSpec((None, 1, MOD_TN), lambda l, j: (l, 0, j)),
        ],
        out_specs=pl.BlockSpec((None, MOD_ROWS, MOD_TN), lambda l, j: (l, 0, j)),
        out_shape=jax.ShapeDtypeStruct((DEPTH, MOD_ROWS, n_out), F32),
        compiler_params=_params(("arbitrary", "arbitrary"), 40 * 1024 * 1024),
        name="adaln_modulation",
    )(cond, ada_w, ada_b.reshape(DEPTH, 1, n_out))
    return out.reshape(DEPTH, MOD_ROWS, 6, D)


def _rope_tables():
    n_rows = T_LAT // GRID_W
    row = jnp.repeat(jnp.arange(n_rows, dtype=F32), GRID_W)
    col = jnp.tile(jnp.arange(GRID_W, dtype=F32), n_rows)
    n_freq = HEAD_DIM // 4
    inv_freq = ROPE_THETA ** (-jnp.arange(n_freq, dtype=F32) / n_freq)
    ang = jnp.stack([row[:, None] * inv_freq, col[:, None] * inv_freq], axis=1)
    cos, sin = jnp.cos(ang), jnp.sin(ang)
    zero = jnp.zeros_like(sin)
    cos64 = jnp.concatenate([cos, cos], axis=-1).reshape(T_LAT, HEAD_DIM)
    sin_up = jnp.concatenate([-sin, zero], axis=-1).reshape(T_LAT, HEAD_DIM)
    sin_dn = jnp.concatenate([zero, sin], axis=-1).reshape(T_LAT, HEAD_DIM)
    rep = LANES // HEAD_DIM
    return tuple(jnp.tile(t, (1, rep)) for t in (cos64, sin_up, sin_dn))


def _qkv_kernel(xc_ref, xl_ref, mod_ref, g_ref, w_ref, cos_ref, sup_ref, sdn_ref,
                q_ref, k_ref, v_ref, kc_ref, vc_ref):
    nct = N_CTX // TM_QKV
    i = pl.program_id(0)
    x = jnp.where(i < nct, xc_ref[...], xl_ref[...])
    h = _norm_mod(x, g_ref[...], mod_ref[0:1, :], mod_ref[1:2, :])
    qkv = jnp.dot(h.astype(BF16), w_ref[...], preferred_element_type=F32)
    q = qkv[:, :D] * (HEAD_DIM ** -0.5)
    k = qkv[:, D:2 * D]
    v = qkv[:, 2 * D:]
    v_ref[...] = v.astype(BF16)

    @pl.when(i < nct)
    def _():
        q_ref[...] = q.astype(BF16)
        k_ref[...] = k.astype(BF16)
        kc_ref[...] = k
        vc_ref[...] = v

    @pl.when(i >= nct)
    def _():
        rep = D // LANES
        cos = jnp.tile(cos_ref[...], (1, rep))
        sup = jnp.tile(sup_ref[...], (1, rep))
        sdn = jnp.tile(sdn_ref[...], (1, rep))

        def rope(z):
            return z * cos + pltpu.roll(z, D - 16, 1) * sup + pltpu.roll(z, 16, 1) * sdn

        q_ref[...] = rope(q).astype(BF16)
        k_ref[...] = rope(k).astype(BF16)


def _qkv(xc, xl, mod, g, w_qkv, tables):
    tm = TM_QKV
    nct = N_CTX // tm
    spec_c, spec_l = _split_specs(tm, D)
    tab_spec = pl.BlockSpec((tm, LANES), lambda i: (jnp.maximum(i - nct, 0) % (T_LAT // tm), 0))
    tok_spec = pl.BlockSpec((tm, D), lambda i: (i, 0))
    ctx_spec = pl.BlockSpec((tm, D), lambda i: (jnp.minimum(i, nct - 1), 0))
    return pl.pallas_call(
        _qkv_kernel,
        grid=(N_TOK // tm,),
        in_specs=[
            spec_c, spec_l,
            pl.BlockSpec((None, 6, D), lambda i: (_mod_row(i, tm), 0, 0)),
            _const_spec((1, D)),
            _const_spec((D, 3 * D)),
            tab_spec, tab_spec, tab_spec,
        ],
        out_specs=[tok_spec, tok_spec, tok_spec, ctx_spec, ctx_spec],
        out_shape=[
            jax.ShapeDtypeStruct((N_TOK, D), BF16),
            jax.ShapeDtypeStruct((N_TOK, D), BF16),
            jax.ShapeDtypeStruct((N_TOK, D), BF16),
            jax.ShapeDtypeStruct((N_CTX, D), F32),
            jax.ShapeDtypeStruct((N_CTX, D), F32),
        ],
        compiler_params=_params(("arbitrary",), 48 * 1024 * 1024),
        name="qkv_rope",
    )(xc, xl, mod, g, w_qkv, *tables)


def _lambda_full(lam_ref, lambda_init):
    d1 = jnp.sum(lam_ref[0:1, :] * lam_ref[1:2, :], axis=-1, keepdims=True)
    d2 = jnp.sum(lam_ref[2:3, :] * lam_ref[3:4, :], axis=-1, keepdims=True)
    return jnp.exp(d1) - jnp.exp(d2) + lambda_init


def _softmax_rows(s):
    m = jnp.max(s, axis=-1, keepdims=True)
    e = jnp.exp(s - m)
    return e * (1.0 / jnp.sum(e, axis=-1, keepdims=True))


def _diff_heads(q_ref, k_ref, v_ref, lam, sg, o_ref, lambda_init):
    tq = q_ref.shape[0]
    first_map = lax.broadcasted_iota(I32, (tq, V_DIM), 1) < HEAD_DIM
    dn = (((1,), (1,)), ((), ()))
    for h in range(N_HEADS):
        sl = slice(h * V_DIM, (h + 1) * V_DIM)
        q = q_ref[:, sl]
        k = k_ref[:, sl]
        v = v_ref[:, sl]
        zero = jnp.zeros_like(q)
        s1 = lax.dot_general(jnp.where(first_map, q, zero), k, dn, preferred_element_type=F32)
        s2 = lax.dot_general(jnp.where(first_map, zero, q), k, dn, preferred_element_type=F32)
        a = _softmax_rows(s1) - lam * _softmax_rows(s2)
        o = jnp.dot(a.astype(BF16), v, preferred_element_type=F32)
        r = lax.rsqrt(jnp.mean(o * o, axis=-1, keepdims=True) + EPS)
        o_ref[:, sl] = (((o * r) * sg) * (1.0 - lambda_init)).astype(BF16)


def _attn_ctx_kernel(q_ref, k_ref, v_ref, lam_ref, sg_ref, o_ref, *, lambda_init):
    lam = _lambda_full(lam_ref, lambda_init)
    _diff_heads(q_ref, k_ref, v_ref, lam, sg_ref[...], o_ref, lambda_init)


def _attn_lat_kernel(q_ref, kl_ref, vl_ref, ck_ref, cv_ref, lam_ref, sg_ref, o_ref,
                     kcat, vcat, *, lambda_init):
    @pl.when(pl.program_id(1) == 0)
    def _():
        kcat[0:PAST, :] = ck_ref[...].astype(BF16)
        kcat[PAST:, :] = kl_ref[...]
        vcat[0:PAST, :] = cv_ref[...].astype(BF16)
        vcat[PAST:, :] = vl_ref[...]

    lam = _lambda_full(lam_ref, lambda_init)
    _diff_heads(q_ref, kcat, vcat, lam, sg_ref[...], o_ref, lambda_init)


def _attention(q, k, v, cache_k, cache_v, lam4, sg, lambda_init):
    row_spec = pl.BlockSpec((T_CTX, D), lambda b: (b, 0))
    o_c = pl.pallas_call(
        functools.partial(_attn_ctx_kernel, lambda_init=lambda_init),
        grid=(N_CTX_SEQ,),
        in_specs=[row_spec, row_spec, row_spec, _const_spec((4, HEAD_DIM)), _const_spec((1, V_DIM))],
        out_specs=row_spec,
        out_shape=jax.ShapeDtypeStruct((N_CTX, D), BF16),
        compiler_params=_params(("arbitrary",), 32 * 1024 * 1024),
        name="diff_attn_ctx",
    )(q, k, v, lam4, sg)

    nq = T_LAT // TQ_LAT
    lat0 = N_CTX // TQ_LAT
    kv_spec = pl.BlockSpec((T_LAT, D), lambda b, j: (N_CTX // T_LAT + b, 0))
    cache_spec = pl.BlockSpec((None, PAST, D), lambda b, j: (b, 0, 0))
    o_l = pl.pallas_call(
        functools.partial(_attn_lat_kernel, lambda_init=lambda_init),
        grid=(N_LAT_SEQ, nq),
        in_specs=[
            pl.BlockSpec((TQ_LAT, D), lambda b, j: (lat0 + b * nq + j, 0)),
            kv_spec, kv_spec, cache_spec, cache_spec,
            _const_spec((4, HEAD_DIM)), _const_spec((1, V_DIM)),
        ],
        out_specs=pl.BlockSpec((TQ_LAT, D), lambda b, j: (b * nq + j, 0)),
        out_shape=jax.ShapeDtypeStruct((N_LAT, D), BF16),
        scratch_shapes=[pltpu.VMEM((PAST + T_LAT, D), BF16), pltpu.VMEM((PAST + T_LAT, D), BF16)],
        compiler_params=_params(("arbitrary", "arbitrary"), 48 * 1024 * 1024),
        name="diff_attn_lat",
    )(q, k, v, cache_k, cache_v, lam4, sg)
    return o_c, o_l


def _pw1_glu_kernel(x_ref, mod_ref, g_ref, w_ref, b_ref, u_ref):
    h = _norm_mod(x_ref[...], g_ref[...], mod_ref[0:1, :], mod_ref[1:2, :])
    z = jnp.dot(h.astype(BF16), w_ref[...], preferred_element_type=F32) + b_ref[...]
    u_ref[...] = z[:, :D] * _sigmoid(z[:, D:])


def _pw1_glu(x, mod, g, w, b):
    tm = BLK
    return pl.pallas_call(
        _pw1_glu_kernel,
        grid=(N_TOK // tm,),
        in_specs=[
            pl.BlockSpec((tm, D), lambda i: (i, 0)),
            pl.BlockSpec((None, 6, D), lambda i: (_mod_row(i, tm), 0, 0)),
            _const_spec((1, D)),
            _const_spec((D, 2 * D)),
            _const_spec((1, 2 * D)),
        ],
        out_specs=pl.BlockSpec((tm, D), lambda i: (i, 0)),
        out_shape=jax.ShapeDtypeStruct((N_TOK, D), F32),
        compiler_params=_params(("arbitrary",), 48 * 1024 * 1024),
        name="conv_pw1_glu",
    )(x, mod, g, w, b)


CONV_PAD = 16
CONV_TT = 64


def _dwconv_kernel(u_ref, w_ref, b_ref, o_ref, up):
    t_len = u_ref.shape[0]
    up[0:CONV_PAD, :] = jnp.zeros((CONV_PAD, LANES), F32)
    up[CONV_PAD:CONV_PAD + t_len, :] = u_ref[...]
    up[CONV_PAD + t_len:, :] = jnp.zeros((CONV_PAD, LANES), F32)
    first = CONV_PAD - CONV_W // 2
    for c in range(t_len // CONV_TT):
        t0 = c * CONV_TT
        acc = jnp.zeros((CONV_TT, LANES), F32)
        for tap in range(CONV_W):
            acc = acc + up[t0 + first + tap:t0 + first + tap + CONV_TT, :] * w_ref[tap:tap + 1, :]
        o_ref[t0:t0 + CONV_TT, :] = acc + b_ref[...]


def _dwconv(u, w_dw32, b_dw, t_len, n_seq, first_blk):
    return pl.pallas_call(
        _dwconv_kernel,
        grid=(n_seq, D // LANES),
        in_specs=[
            pl.BlockSpec((t_len, LANES), lambda s, c: (first_blk + s, c)),
            pl.BlockSpec((32, LANES), lambda s, c: (0, c)),
            pl.BlockSpec((1, LANES), lambda s, c: (0, c)),
        ],
        out_specs=pl.BlockSpec((t_len, LANES), lambda s, c: (s, c)),
        out_shape=jax.ShapeDtypeStruct((n_seq * t_len, D), F32),
        scratch_shapes=[pltpu.VMEM((t_len + 2 * CONV_PAD, LANES), F32)],
        compiler_params=_params(("arbitrary", "arbitrary"), 32 * 1024 * 1024),
        name="depthwise_conv_t%d" % t_len,
    )(u, w_dw32, b_dw)


def _route_and_sort(h2, rw_ref, rb_ref, xs_ref, meta_ref, cnt_ref):
    dn = (((1,), (1,)), ((), ()))
    logits = lax.dot_general(rw_ref[...], h2, dn, precision=HIGHEST, preferred_element_type=F32)
    scores = _sigmoid(logits[0:N_EXPERTS, :])
    sel = scores + rb_ref[...]
    row = lambda a, r: a[r:r + 1, :]
    one = lambda c: jnp.where(c, 1.0, 0.0)

    in2 = []
    gscore = []
    for g in range(N_EXPERTS // GROUP):
        a, b, c, d = (row(sel, GROUP * g + j) for j in range(GROUP))
        ranks = [
            one(b > a) + one(c > a) + one(d > a),
            one(a >= b) + one(c > b) + one(d > b),
            one(a >= c) + one(b >= c) + one(d > c),
            one(a >= d) + one(b >= d) + one(c >= d),
        ]
        keep = [r < 2.0 for r in ranks]
        in2.extend(keep)
        gscore.append(sum(jnp.where(kp, val, 0.0) for kp, val in zip(keep, (a, b, c, d))))

    best = jnp.zeros_like(gscore[0])
    best_val = gscore[0]
    for g in range(1, N_EXPERTS // GROUP):
        upd = gscore[g] > best_val
        best = jnp.where(upd, float(g), best)
        best_val = jnp.where(upd, gscore[g], best_val)

    chosen = [jnp.logical_and(best == float(e // GROUP), in2[e]) for e in range(N_EXPERTS)]
    den = sum(jnp.where(ch, row(scores, e), 0.0) for e, ch in enumerate(chosen))
    gate = [row(scores, e) / den for e in range(N_EXPERTS)]

    chosen_f = jnp.concatenate([one(ch) for ch in chosen], axis=0)
    earlier = (lax.broadcasted_iota(I32, (BLK, BLK), 0) < lax.broadcasted_iota(I32, (BLK, BLK), 1))
    rank = jnp.dot(chosen_f.astype(BF16), jnp.where(earlier, 1.0, 0.0).astype(BF16),
                   preferred_element_type=F32)
    cnt = jnp.sum(chosen_f, axis=1, keepdims=True).astype(I32)
    padded = (((cnt + (CHUNK - 1)) >> CHUNK_SHIFT) << CHUNK_SHIFT).astype(F32)

    base = jnp.zeros((1, 1), F32)
    seen = jnp.zeros_like(gscore[0])
    pos_lo = jnp.zeros_like(seen)
    pos_hi = jnp.zeros_like(seen)
    w_lo = jnp.zeros_like(seen)
    w_hi = jnp.zeros_like(seen)
    for e in range(N_EXPERTS):
        pos_e = base + row(rank, e)
        is_lo = jnp.logical_and(chosen[e], seen == 0.0)
        is_hi = jnp.logical_and(chosen[e], seen == 1.0)
        pos_lo = jnp.where(is_lo, pos_e, pos_lo)
        pos_hi = jnp.where(is_hi, pos_e, pos_hi)
        w_lo = jnp.where(is_lo, gate[e], w_lo)
        w_hi = jnp.where(is_hi, gate[e], w_hi)
        seen = seen + one(chosen[e])
        base = base + padded[e:e + 1, :]

    slot = lax.broadcasted_iota(I32, (BLK_ROWS, BLK), 0)
    hit = jnp.logical_or(slot == pos_lo.astype(I32), slot == pos_hi.astype(I32))
    perm = jnp.where(hit, 1.0, 0.0).astype(BF16)
    xs_ref[...] = jnp.dot(perm, h2.astype(BF16), preferred_element_type=F32).astype(BF16)

    meta_t = jnp.concatenate([pos_lo, pos_hi, w_lo, w_hi, jnp.zeros((LANES - 4, BLK), F32)], axis=0)
    meta_ref[...] = meta_t.T
    cnt_ref[...] = jnp.broadcast_to(cnt, (N_EXPERTS, LANES))


def _mix_route_kernel(*refs, split_x, ln_silu):
    refs = list(refs)
    i = pl.program_id(0)
    nct = N_CTX // BLK
    a_c, a_l = refs.pop(0), refs.pop(0)
    a = jnp.where(i < nct, a_c[...], a_l[...])
    if split_x:
        x_c, x_l = refs.pop(0), refs.pop(0)
        x = jnp.where(i < nct, x_c[...], x_l[...])
    else:
        x = refs.pop(0)[...]
    if ln_silu:
        lng, lnb = refs.pop(0), refs.pop(0)
        mu = jnp.mean(a, axis=-1, keepdims=True)
        var = jnp.mean(jnp.square(a - mu), axis=-1, keepdims=True)
        a = ((a - mu) * lax.rsqrt(var + EPS)) * lng[...] + lnb[...]
        a = a * _sigmoid(a)
    mod_ref, w_ref, b_ref, g2_ref, rw_ref, rb_ref, xn_ref, xs_ref, meta_ref, cnt_ref = refs
    mix = jnp.dot(a.astype(BF16), w_ref[...], preferred_element_type=F32) + b_ref[...]
    x_new = x + mod_ref[2:3, :] * mix
    xn_ref[...] = x_new
    h2 = _norm_mod(x_new, g2_ref[...], mod_ref[3:4, :], mod_ref[4:5, :])
    _route_and_sort(h2, rw_ref, rb_ref, xs_ref, meta_ref, cnt_ref)


def _mix_route(a_parts, x_parts, ln, mod, w, b, g2, rw_t, rb):
    split_x = len(x_parts) == 2
    a_dtype_width = D
    spec_c, spec_l = _split_specs(BLK, a_dtype_width)
    in_specs = [spec_c, spec_l]
    args = list(a_parts)
    if split_x:
        in_specs += [spec_c, spec_l]
    else:
        in_specs += [pl.BlockSpec((BLK, D), lambda i: (i, 0))]
    args += list(x_parts)
    if ln is not None:
        in_specs += [_const_spec((1, D)), _const_spec((1, D))]
        args += list(ln)
    in_specs += [
        pl.BlockSpec((None, 6, D), lambda i: (_mod_row(i, BLK), 0, 0)),
        _const_spec((D, D)),
        _const_spec((1, D)),
        _const_spec((1, D)),
        _const_spec((LANES, D)),
        _const_spec((N_EXPERTS, 1)),
    ]
    args += [mod, w, b, g2, rw_t, rb]
    return pl.pallas_call(
        functools.partial(_mix_route_kernel, split_x=split_x, ln_silu=ln is not None),
        grid=(N_BLK,),
        in_specs=in_specs,
        out_specs=[
            pl.BlockSpec((BLK, D), lambda i: (i, 0)),
            pl.BlockSpec((None, BLK_ROWS, D), lambda i: (i, 0, 0)),
            pl.BlockSpec((BLK, LANES), lambda i: (i, 0)),
            pl.BlockSpec((None, N_EXPERTS, LANES), lambda i: (i, 0, 0)),
        ],
        out_shape=[
            jax.ShapeDtypeStruct((N_TOK, D), F32),
            jax.ShapeDtypeStruct((N_BLK, BLK_ROWS, D), BF16),
            jax.ShapeDtypeStruct((N_TOK, LANES), F32),
            jax.ShapeDtypeStruct((N_BLK, N_EXPERTS, LANES), I32),
        ],
        compiler_params=_params(("arbitrary",), 48 * 1024 * 1024),
        name="mixer_out_route_sort",
    )(*args)


def _chunk_table(cnt):
    nch = (cnt + (CHUNK - 1)) // CHUNK
    start_in_blk = jnp.cumsum(nch, axis=1) - nch
    first = (jnp.arange(N_BLK, dtype=I32) * BLK_CHUNKS)[:, None] + start_in_blk
    ends = jnp.cumsum(nch, axis=0)
    total = ends[-1]
    j = jnp.arange(MAX_CHUNKS_PER_EXPERT, dtype=I32)
    blk = jnp.sum((j[None, :, None] >= ends.T[:, None, :]).astype(I32), axis=-1)
    blk = jnp.minimum(blk, N_BLK - 1)
    begin = jnp.take_along_axis((ends - nch).T, blk, axis=1)
    ids = jnp.take_along_axis(first.T, blk, axis=1) + (j[None, :] - begin)
    ids = jnp.where(j[None, :] < total[:, None], ids, N_CHUNKS)
    n_tiles = (total + (TILE_CHUNKS - 1)) // TILE_CHUNKS
    return n_tiles.astype(I32), ids.reshape(-1).astype(I32)


def _expert_kernel(nt_ref, ids_ref, xs_hbm, wg_ref, wu_ref, wd_ref, ys_hbm, rows, wg, wu, wd, sem):
    e = pl.program_id(0)

    @pl.when(e == 0)
    def _():
        load = pltpu.make_async_copy(xs_hbm, rows.at[pl.ds(0, N_CHUNKS)], sem.at[0])
        load.start()
        rows[N_CHUNKS] = jnp.zeros((CHUNK, D), BF16)
        load.wait()

    wg[...] = wg_ref[...].astype(BF16)
    wu[...] = wu_ref[...].astype(BF16)
    wd[...] = wd_ref[...].astype(BF16)

    def tile(t, carry):
        first = e * MAX_CHUNKS_PER_EXPERT + t * TILE_CHUNKS
        ids = [ids_ref[first + j] for j in range(TILE_CHUNKS)]
        x = jnp.concatenate([rows[c] for c in ids], axis=0)
        ga = jnp.dot(x, wg[...], preferred_element_type=F32)
        up = jnp.dot(x, wu[...], preferred_element_type=F32)
        act = (ga * _sigmoid(ga)) * up
        y = jnp.dot(act.astype(BF16), wd[...], preferred_element_type=F32).astype(BF16)
        for j, c in enumerate(ids):
            rows[c] = y[j * CHUNK:(j + 1) * CHUNK, :]
        return carry

    lax.fori_loop(0, nt_ref[e], tile, 0)

    @pl.when(e == N_EXPERTS - 1)
    def _():
        store = pltpu.make_async_copy(rows.at[pl.ds(0, N_CHUNKS)], ys_hbm, sem.at[1])
        store.start()
        store.wait()


def _experts(xs, n_tiles, ids, w_gate, w_up, w_down):
    grid_spec = pltpu.PrefetchScalarGridSpec(
        num_scalar_prefetch=2,
        grid=(N_EXPERTS,),
        in_specs=[
            pl.BlockSpec(memory_space=pl.ANY),
            pl.BlockSpec((None, D, D_EXPERT), lambda e, *_: (e, 0, 0)),
            pl.BlockSpec((None, D, D_EXPERT), lambda e, *_: (e, 0, 0)),
            pl.BlockSpec((None, D_EXPERT, D), lambda e, *_: (e, 0, 0)),
        ],
        out_specs=pl.BlockSpec(memory_space=pl.ANY),
        scratch_shapes=[
            pltpu.VMEM((N_CHUNKS + 1, CHUNK, D), BF16),
            pltpu.VMEM((D, D_EXPERT), BF16),
            pltpu.VMEM((D, D_EXPERT), BF16),
            pltpu.VMEM((D_EXPERT, D), BF16),
            pltpu.SemaphoreType.DMA((2,)),
        ],
    )
    return pl.pallas_call(
        _expert_kernel,
        grid_spec=grid_spec,
        out_shape=jax.ShapeDtypeStruct((N_CHUNKS, CHUNK, D), BF16),
        compiler_params=_params(("arbitrary",), VMEM_LIMIT),
        name="moe_experts",
    )(n_tiles, ids, xs, w_gate, w_up, w_down)


def _combine_kernel(*refs, final):
    ys_ref, meta_ref, x_ref, mod_ref = refs[:4]
    meta = meta_ref[...]
    pos_lo = meta[:, 0:1].astype(I32)
    pos_hi = meta[:, 1:2].astype(I32)
    slot = lax.broadcasted_iota(I32, (BLK, BLK_ROWS), 1)
    gmat = jnp.where(slot == pos_lo, meta[:, 2:3], 0.0) + jnp.where(slot == pos_hi, meta[:, 3:4], 0.0)
    y = jnp.dot(gmat.astype(BF16), ys_ref[...], preferred_element_type=F32)
    x_out = x_ref[...] + mod_ref[5:6, :] * y
    if not final:
        refs[4][...] = x_out
        return
    fg_ref, yc_ref, yl_ref = refs[4:]
    ms = jnp.mean(x_out * x_out, axis=-1, keepdims=True)
    out = (x_out * lax.rsqrt(ms + EPS)) * fg_ref[...]
    nct = N_CTX // BLK

    @pl.when(pl.program_id(0) < nct)
    def _():
        yc_ref[...] = out

    @pl.when(pl.program_id(0) >= nct)
    def _():
        yl_ref[...] = out


def _combine(ys, meta, x, mod, final_g=None):
    final = final_g is not None
    in_specs = [
        pl.BlockSpec((None, BLK_ROWS, D), lambda i: (i, 0, 0)),
        pl.BlockSpec((BLK, LANES), lambda i: (i, 0)),
        pl.BlockSpec((BLK, D), lambda i: (i, 0)),
        pl.BlockSpec((None, 6, D), lambda i: (_mod_row(i, BLK), 0, 0)),
    ]
    args = [ys, meta, x, mod]
    if final:
        in_specs.append(_const_spec((1, D)))
        args.append(final_g)
        out_specs = list(_split_specs(BLK, D))
        out_shape = [jax.ShapeDtypeStruct((N_CTX, D), F32), jax.ShapeDtypeStruct((N_LAT, D), F32)]
    else:
        out_specs = pl.BlockSpec((BLK, D), lambda i: (i, 0))
        out_shape = jax.ShapeDtypeStruct((N_TOK, D), F32)
    return pl.pallas_call(
        functools.partial(_combine_kernel, final=final),
        grid=(N_BLK,),
        in_specs=in_specs,
        out_specs=out_specs,
        out_shape=out_shape,
        compiler_params=_params(("arbitrary",), 48 * 1024 * 1024),
        name="moe_combine_final" if final else "moe_combine",
    )(*args)


def _moe(a_parts, x_parts, ln, mod, w, b, g2, rw_t, rb, w_gate, w_up, w_down, final_g=None):
    x_new, xs, meta, cnt = _mix_route(a_parts, x_parts, ln, mod, w, b, g2, rw_t, rb)
    n_tiles, ids = _chunk_table(cnt[:, :, 0])
    ys = _experts(xs.reshape(N_CHUNKS, CHUNK, D), n_tiles, ids, w_gate, w_up, w_down)
    return _combine(ys.reshape(N_BLK, BLK_ROWS, D), meta, x_new, mod, final_g)


def kernel(x_prompt, x_sample, cache_k, cache_v, c, c_ctx, norm1_g, norm2_g, ada_w, ada_b, attn_w_qkv, attn_w_o, attn_lambda_q1, attn_lambda_k1, attn_lambda_q2, attn_lambda_k2, attn_subln_g, conv_w_pw1, conv_b_pw1, conv_w_dw, conv_b_dw, conv_ln_g, conv_ln_b, conv_w_pw2, conv_b_pw2, router_w, router_bias, moe_w_gate, moe_w_up, moe_w_down, final_norm_g):
    xc = x_prompt.reshape(N_CTX, D)
    xl = x_sample.reshape(N_LAT, D)
    cond = jnp.concatenate([c_ctx[None, :], c, jnp.zeros((MOD_ROWS - 1 - N_LAT_SEQ, D), F32)], axis=0)
    mod = _modulation(cond, ada_w, ada_b)

    rw_t = jnp.zeros((LANES, D), F32).at[:N_EXPERTS].set(router_w.T)
    rb = router_bias.reshape(N_EXPERTS, 1)
    zero_bias = jnp.zeros((1, D), F32)

    lambda_init = 0.8 - 0.6 * math.exp(-0.3 * 0)
    q, k, v, kc, vc = _qkv(xc, xl, mod[0], norm1_g[0:1], attn_w_qkv[0].astype(BF16), _rope_tables())
    lam4 = jnp.stack([attn_lambda_q1[0], attn_lambda_k1[0], attn_lambda_q2[0], attn_lambda_k2[0]])
    o_c, o_l = _attention(q, k, v, cache_k[:, 0].reshape(N_LAT_SEQ, PAST, D),
                          cache_v[:, 0].reshape(N_LAT_SEQ, PAST, D), lam4,
                          attn_subln_g[0:1], lambda_init)
    x1 = _moe((o_c, o_l), (xc, xl), None, mod[0], attn_w_o[0].astype(BF16), zero_bias,
              norm2_g[0:1], rw_t, rb, moe_w_gate[0], moe_w_up[0], moe_w_down[0])

    u = _pw1_glu(x1, mod[1], norm1_g[1:2], conv_w_pw1[0].astype(BF16), conv_b_pw1[0:1])
    w_dw32 = jnp.concatenate([conv_w_dw[0], jnp.zeros((32 - CONV_W, D), F32)], axis=0)
    cv_c = _dwconv(u, w_dw32, conv_b_dw[0:1], T_CTX, N_CTX_SEQ, 0)
    cv_l = _dwconv(u, w_dw32, conv_b_dw[0:1], T_LAT, N_LAT_SEQ, N_CTX // T_LAT)
    y_c, y_l = _moe((cv_c, cv_l), (x1,), (conv_ln_g[0:1], conv_ln_b[0:1]), mod[1],
                    conv_w_pw2[0].astype(BF16), conv_b_pw2[0:1], norm2_g[1:2], rw_t, rb,
                    moe_w_gate[1], moe_w_up[1], moe_w_down[1], final_norm_g.reshape(1, D))

    y_prompt = y_c.reshape(N_CTX_SEQ, T_CTX, D)
    y_sample = y_l.reshape(N_LAT_SEQ, T_LAT, D)
    new_k = kc.reshape(N_CTX_SEQ, 1, T_CTX, N_HEADS, 2 * HEAD_DIM)
    new_v = vc.reshape(N_CTX_SEQ, 1, T_CTX, N_HEADS, V_DIM)
    return (y_prompt, y_sample, new_k, new_v)
```
